```python
import math
import jax, jax.numpy as jnp
from jax import lax
import numpy as np

D_MODEL = 1024
BATCH = 4
SEQ = 4096
DEPTH = 2
DEC_BATCH = 32
DEC_SEQ = 1
PAST_LEN = 8192
PAGE_SIZE = 128

N_META = 16
H_A = 4
DK_A = 128
DV_A = 128
W_A = H_A * DV_A
QKV_A = 2 * H_A * DK_A + H_A * DV_A
CONV_W = 4
GDN_CHUNK = 64
H_B = 8
DH_B = 64
W_B = H_B * DH_B
Q_BLOCK = 128
HC = 16
G_C = 32
P_C = 64
W_C = G_C * HC
PEER_HEADS = 8
N_KEYS = 128
N_EXPERTS = N_KEYS * N_KEYS
PEER_TOPK = 16
PEER_DK = 128
PEER_DK_HALF = PEER_DK // 2
PEER_BLOCK = 128
IN_SPLITS = (QKV_A, H_A, H_A, W_A, W_B, W_B, W_B, H_B, W_C, D_MODEL, D_MODEL, D_MODEL)
IN_TOTAL = sum(IN_SPLITS)
EPS = 1e-6

kernel_name = 'hybrid_gdn_fox_s5_peer_step'


def rms_norm(x, w):
    xf = x.astype(jnp.float32)
    y = xf * lax.rsqrt(jnp.mean(xf * xf, axis=-1, keepdims=True) + EPS)
    return (y * w.astype(jnp.float32)).astype(x.dtype)


def l2_norm(x):
    xf = x.astype(jnp.float32)
    return xf * lax.rsqrt(jnp.sum(xf * xf, axis=-1, keepdims=True) + EPS)


def short_causal_conv(x, buf, w):
    t_len = x.shape[1]
    xp = jnp.concatenate([buf.astype(x.dtype), x], axis=1)
    y = w[0] * xp[:, 0:t_len]
    for j in range(1, CONV_W):
        y = y + w[j] * xp[:, j:j + t_len]
    return y, xp[:, -(CONV_W - 1):]


def gdn_chunk(S, q, k, v, g, beta):
    L = q.shape[2]
    gc = jnp.cumsum(g, axis=-1)
    incl = jnp.tril(jnp.ones((L, L), dtype=bool))
    strict = jnp.tril(jnp.ones((L, L), dtype=bool), -1)
    diff = gc[..., :, None] - gc[..., None, :]
    decay = jnp.where(incl, jnp.exp(jnp.where(incl, diff, 0.0)), 0.0)
    kb = k * beta[..., None]
    a_mat = jnp.where(strict, jnp.einsum('bhid,bhjd->bhij', kb, k) * decay, 0.0)
    eye = jnp.eye(L, dtype=q.dtype)
    t_inv = lax.linalg.triangular_solve(eye + a_mat, jnp.broadcast_to(eye, a_mat.shape),
                                        left_side=True, lower=True, unit_diagonal=True)
    u = jnp.einsum('bhij,bhjd->bhid', t_inv, v * beta[..., None])
    w = jnp.einsum('bhij,bhjd->bhid', t_inv, kb * jnp.exp(gc)[..., None])
    v_new = u - jnp.einsum('bhik,bhkd->bhid', w, S)
    attn = jnp.einsum('bhid,bhjd->bhij', q, k) * decay
    o = (jnp.einsum('bhik,bhkd->bhid', q * jnp.exp(gc)[..., None], S)
         + jnp.einsum('bhij,bhjd->bhid', attn, v_new))
    k_dec = k * jnp.exp(gc[..., -1:] - gc)[..., None]
    S_new = S * jnp.exp(gc[..., -1])[..., None, None] + jnp.einsum('bhlk,bhld->bhkd', k_dec, v_new)
    return S_new, o


def gdn_recurrence(q, k, v, g, beta, S0, n_lead):
    t_len = q.shape[2]
    S, o = gdn_chunk(S0, q[:, :, :n_lead], k[:, :, :n_lead], v[:, :, :n_lead],
                     g[:, :, :n_lead], beta[:, :, :n_lead])
    rest = t_len - n_lead
    if rest > 0:
        n_c = rest // GDN_CHUNK

        def to_chunks(a):
            a = a[:, :, n_lead:]
            a = a.reshape(a.shape[:2] + (n_c, GDN_CHUNK) + a.shape[3:])
            return jnp.moveaxis(a, 2, 0)

        xs = (to_chunks(q), to_chunks(k), to_chunks(v), to_chunks(g), to_chunks(beta))
        S, o_rest = lax.scan(lambda s, c: gdn_chunk(s, c[0], c[1], c[2], c[3], c[4]), S, xs)
        o_rest = jnp.moveaxis(o_rest, 0, 2)
        o_rest = o_rest.reshape(o_rest.shape[:2] + (rest, o_rest.shape[-1]))
        o = jnp.concatenate([o, o_rest], axis=2)
    return o, S


def fox_attend(q, cq, qpos, k, v, c):
    kpos = jnp.arange(k.shape[1])
    s = jnp.einsum('bqhd,bkhd->bhqk', q, k).astype(jnp.float32) * (DH_B ** -0.5)
    s = s + (jnp.swapaxes(cq, 1, 2)[..., :, None] - jnp.swapaxes(c, 1, 2)[..., None, :])
    s = jnp.where(kpos[None, :] <= qpos[:, None], s, -jnp.inf)
    p = jax.nn.softmax(s, axis=-1)
    return jnp.einsum('bhqk,bkhd->bqhd', p.astype(v.dtype), v)


def fox_prompt(q, k, v, logf):
    B, t_len = q.shape[0], q.shape[1]
    c = jnp.cumsum(logf.astype(jnp.float32), axis=1)
    pos = jnp.arange(t_len)
    o_meta = fox_attend(q[:, :N_META], c[:, :N_META], pos[:N_META], k, v, c)
    n_real = t_len - N_META
    nb = n_real // Q_BLOCK
    qb = jnp.moveaxis(q[:, N_META:].reshape(B, nb, Q_BLOCK, H_B, DH_B), 1, 0)
    cb = jnp.moveaxis(c[:, N_META:].reshape(B, nb, Q_BLOCK, H_B), 1, 0)
    pb = pos[N_META:].reshape(nb, Q_BLOCK)
    o = lax.map(lambda a: fox_attend(a[0], a[1], a[2], k, v, c), (qb, cb, pb))
    o = jnp.moveaxis(o, 0, 1).reshape(B, n_real, H_B, DH_B)
    return jnp.concatenate([o_meta, o], axis=1)


def fox_sample(q, k, v, logf, past_k, past_v, past_logf):
    n_past = past_k.shape[1]
    k_all = jnp.concatenate([past_k.astype(k.dtype), k], axis=1)
    v_all = jnp.concatenate([past_v.astype(v.dtype), v], axis=1)
    c = jnp.cumsum(jnp.concatenate([past_logf.astype(jnp.float32), logf], axis=1), axis=1)
    qpos = n_past + jnp.arange(q.shape[1])
    return fox_attend(q, c[:, n_past:], qpos, k_all, v_all, c)


def s5_combine(e1, e2):
    ar1, ai1, br1, bi1 = e1
    ar2, ai2, br2, bi2 = e2
    return (ar2 * ar1 - ai2 * ai1, ar2 * ai1 + ai2 * ar1,
            ar2 * br1 - ai2 * bi1 + br2, ar2 * bi1 + ai2 * br1 + bi2)


def s5_mixer(u, lp, h0_re, h0_im):
    f32 = jnp.float32
    B, t_len = u.shape[0], u.shape[1]
    uf = u.astype(f32).reshape(B, t_len, G_C, HC)
    a_re = lp['ssm_A_re'].astype(f32)
    a_im = lp['ssm_A_im'].astype(f32)
    dt = jnp.exp(lp['ssm_log_dt'].astype(f32))[:, None]
    mag = jnp.exp(a_re * dt)
    lam_re = mag * jnp.cos(a_im * dt)
    lam_im = mag * jnp.sin(a_im * dt)
    den = a_re * a_re + a_im * a_im
    coef_re = ((lam_re - 1.0) * a_re + lam_im * a_im) / den
    coef_im = (lam_im * a_re - (lam_re - 1.0) * a_im) / den
    b_re = lp['ssm_B_re'].astype(f32)
    b_im = lp['ssm_B_im'].astype(f32)
    bb_re = coef_re[..., None] * b_re - coef_im[..., None] * b_im
    bb_im = coef_re[..., None] * b_im + coef_im[..., None] * b_re
    bu_re = jnp.einsum('btgh,gph->btgp', uf, bb_re)
    bu_im = jnp.einsum('btgh,gph->btgp', uf, bb_im)
    shp = (1, t_len, G_C, P_C)
    a_cum_re, a_cum_im, h_re, h_im = lax.associative_scan(
        s5_combine, (jnp.broadcast_to(lam_re, shp), jnp.broadcast_to(lam_im, shp), bu_re, bu_im), axis=1)
    if h0_re is not None:
        r0 = h0_re.astype(f32)[:, None]
        i0 = h0_im.astype(f32)[:, None]
        h_re, h_im = (h_re + a_cum_re * r0 - a_cum_im * i0, h_im + a_cum_re * i0 + a_cum_im * r0)
    y = (jnp.einsum('btgp,ghp->btgh', h_re, lp['ssm_C_re'].astype(f32))
         - jnp.einsum('btgp,ghp->btgh', h_im, lp['ssm_C_im'].astype(f32))
         + lp['ssm_D'].astype(f32) * uf)
    y = jax.nn.gelu(y.reshape(B, t_len, W_C))
    y = y * jax.nn.sigmoid(y @ lp['ssm_glu_w'].astype(f32) + lp['ssm_glu_b'].astype(f32))
    return y.astype(u.dtype), h_re[:, -1], h_im[:, -1]


def peer_block(hb, wq, keys, U, V):
    n = hb.shape[0]
    q = (hb @ wq).reshape(n, PEER_HEADS, 2, PEER_DK_HALF)
    s = jnp.einsum('nhcd,hckd->nhck', q, keys).astype(jnp.float32)
    s1, i1 = lax.top_k(s[:, :, 0], PEER_TOPK)
    s2, i2 = lax.top_k(s[:, :, 1], PEER_TOPK)
    cand = (s1[..., :, None] + s2[..., None, :]).reshape(n, PEER_HEADS, PEER_TOPK * PEER_TOPK)
    cidx = (i1[..., :, None] * N_KEYS + i2[..., None, :]).reshape(n, PEER_HEADS, PEER_TOPK * PEER_TOPK)
    top_s, top_pos = lax.top_k(cand, PEER_TOPK)
    eidx = jnp.take_along_axis(cidx, top_pos, axis=-1)
    gate = jax.nn.softmax(top_s, axis=-1)
    act = jax.nn.gelu(jnp.einsum('nhkd,nd->nhk', U[eidx], hb).astype(jnp.float32))
    return jnp.einsum('nhk,nhkd->nd', (gate * act).astype(hb.dtype), V[eidx])


def peer_ffn(h, wq, keys, U, V):
    B, t_len, d = h.shape
    hf = h.reshape(-1, d)
    n = hf.shape[0]
    n_pad = (-n) % PEER_BLOCK
    hf = jnp.pad(hf, ((0, n_pad), (0, 0)))
    out = lax.map(lambda hb: peer_block(hb, wq, keys, U, V), hf.reshape(-1, PEER_BLOCK, d))
    return out.reshape(-1, d)[:n].reshape(B, t_len, d)


def hybrid_layer(x, lp, st):
    f32 = jnp.float32
    B, t_len = x.shape[0], x.shape[1]
    h = rms_norm(x, lp['norm_mix_w'])
    z = h @ lp['w_in']
    parts = []
    off = 0
    for size in IN_SPLITS:
        parts.append(z[..., off:off + size])
        off += size
    qkv_a, a_a, b_a, z_a, q_b, k_b, v_b, f_b, u_c, g_a, g_b, g_c = parts

    conv_buf = jnp.zeros((B, CONV_W - 1, QKV_A), x.dtype) if st is None else st['gdn_conv']
    qkv_c, conv_new = short_causal_conv(qkv_a, conv_buf, lp['gdn_conv_w'])
    qkv_c = jax.nn.silu(qkv_c)
    qa = l2_norm(qkv_c[..., :H_A * DK_A].reshape(B, t_len, H_A, DK_A)) * (DK_A ** -0.5)
    ka = l2_norm(qkv_c[..., H_A * DK_A:2 * H_A * DK_A].reshape(B, t_len, H_A, DK_A))
    va = qkv_c[..., 2 * H_A * DK_A:].reshape(B, t_len, H_A, DV_A).astype(f32)
    log_alpha = -jnp.exp(lp['gdn_A_log'].astype(f32)) * jax.nn.softplus(
        a_a.astype(f32) + lp['gdn_dt_bias'].astype(f32))
    beta = jax.nn.sigmoid(b_a.astype(f32))
    S0 = jnp.zeros((B, H_A, DK_A, DV_A), f32) if st is None else st['gdn_S'].astype(f32)
    n_lead = N_META if st is None else t_len
    o_a, S_new = gdn_recurrence(jnp.swapaxes(qa, 1, 2), jnp.swapaxes(ka, 1, 2), jnp.swapaxes(va, 1, 2),
                                jnp.swapaxes(log_alpha, 1, 2), jnp.swapaxes(beta, 1, 2), S0, n_lead)
    o_a = rms_norm(jnp.swapaxes(o_a, 1, 2), lp['gdn_norm_w']) * jax.nn.silu(
        z_a.reshape(B, t_len, H_A, DV_A).astype(f32))
    o_a = o_a.reshape(B, t_len, W_A).astype(x.dtype)

    qb = rms_norm(q_b.reshape(B, t_len, H_B, DH_B), lp['fox_qn_w'])
    kb = rms_norm(k_b.reshape(B, t_len, H_B, DH_B), lp['fox_kn_w'])
    vb = v_b.reshape(B, t_len, H_B, DH_B)
    logf = jax.nn.log_sigmoid(f_b.astype(f32) + lp['fox_f_bias'].astype(f32))
    if st is None:
        o_b = fox_prompt(qb, kb, vb, logf)
    else:
        o_b = fox_sample(qb, kb, vb, logf, st['fox_k'], st['fox_v'], st['fox_logf'])
    o_b = o_b.reshape(B, t_len, W_B)

    if st is None:
        o_c, h_re, h_im = s5_mixer(u_c, lp, None, None)
    else:
        o_c, h_re, h_im = s5_mixer(u_c, lp, st['ssm_re'], st['ssm_im'])

    mix = (jax.nn.sigmoid(g_a) * (o_a @ lp['w_branch_a'])
           + jax.nn.sigmoid(g_b) * (o_b @ lp['w_branch_b'])
           + jax.nn.sigmoid(g_c) * (o_c @ lp['w_branch_c']))
    x = x + (mix @ lp['w_out']).astype(x.dtype)

    x = x + peer_ffn(rms_norm(x, lp['norm_ffn_w']), lp['peer_wq'], lp['peer_keys'],
                     lp['peer_u'], lp['peer_v']).astype(x.dtype)
    return x, (kb, vb, logf, S_new, conv_new, h_re, h_im)


def setup_inputs(seed: int = 0) -> dict:
    key = jax.random.key(seed)
    ks = iter(jax.random.split(key, 64))
    f32 = jnp.float32
    n_pages = PAST_LEN // PAGE_SIZE
    n_phys = (DEC_BATCH * n_pages * 5) // 4

    def nrm(shape, scale=1.0):
        return jax.random.normal(next(ks), shape, f32) * scale

    x_prompt = nrm((BATCH, SEQ, D_MODEL))
    x_sample = nrm((DEC_BATCH, DEC_SEQ, D_MODEL))
    cache_fox_k = nrm((DEPTH, n_phys, PAGE_SIZE, H_B, DH_B))
    cache_fox_v = nrm((DEPTH, n_phys, PAGE_SIZE, H_B, DH_B))
    cache_fox_logf = jax.nn.log_sigmoid(2.5 + nrm((DEPTH, n_phys, PAGE_SIZE, H_B)))
    page_table = jax.random.permutation(next(ks), n_phys)[:DEC_BATCH * n_pages].reshape(
        DEC_BATCH, n_pages).astype(jnp.int32)
    state_gdn = nrm((DEPTH, DEC_BATCH, H_A, DK_A, DV_A), DK_A ** -0.5)
    state_gdn_conv = nrm((DEPTH, DEC_BATCH, CONV_W - 1, QKV_A))
    state_ssm_re = nrm((DEPTH, DEC_BATCH, G_C, P_C), 0.5)
    state_ssm_im = nrm((DEPTH, DEC_BATCH, G_C, P_C), 0.5)
    meta_tokens = nrm((N_META, D_MODEL))
    norm_mix_w = 1.0 + nrm((DEPTH, D_MODEL), 0.02)
    w_in = nrm((DEPTH, D_MODEL, IN_TOTAL), D_MODEL ** -0.5)
    gdn_conv_w = nrm((DEPTH, CONV_W, QKV_A), CONV_W ** -0.5)
    gdn_A_log = jnp.log(jax.random.uniform(next(ks), (DEPTH, H_A), f32, 1.0, 16.0))
    dt_a = jnp.exp(jax.random.uniform(next(ks), (DEPTH, H_A), f32, math.log(1e-3), math.log(1e-1)))
    gdn_dt_bias = dt_a + jnp.log(-jnp.expm1(-dt_a))
    gdn_norm_w = 1.0 + nrm((DEPTH, DV_A), 0.02)
    fox_f_bias = jax.random.uniform(next(ks), (DEPTH, H_B), f32, 1.0, 4.0)
    fox_qn_w = 1.0 + nrm((DEPTH, DH_B), 0.02)
    fox_kn_w = 1.0 + nrm((DEPTH, DH_B), 0.02)
    ssm_A_re = -0.5 * jnp.exp(nrm((DEPTH, G_C, P_C), 0.01))
    ssm_A_im = jnp.pi * jnp.broadcast_to(jnp.arange(P_C, dtype=f32), (DEPTH, G_C, P_C)) * jnp.exp(
        nrm((DEPTH, G_C, P_C), 0.01))
    ssm_log_dt = jax.random.uniform(next(ks), (DEPTH, G_C), f32, math.log(1e-3), math.log(1e-1))
    ssm_B_re = nrm((DEPTH, G_C, P_C, HC), (2 * HC) ** -0.5)
    ssm_B_im = nrm((DEPTH, G_C, P_C, HC), (2 * HC) ** -0.5)
    ssm_C_re = nrm((DEPTH, G_C, HC, P_C), P_C ** -0.5)
    ssm_C_im = nrm((DEPTH, G_C, HC, P_C), P_C ** -0.5)
    ssm_D = nrm((DEPTH, G_C, HC))
    ssm_glu_w = nrm((DEPTH, W_C, W_C), W_C ** -0.5)
    ssm_glu_b = nrm((DEPTH, W_C), 0.01)
    w_branch_a = nrm((DEPTH, W_A, D_MODEL), W_A ** -0.5)
    w_branch_b = nrm((DEPTH, W_B, D_MODEL), W_B ** -0.5)
    w_branch_c = nrm((DEPTH, W_C, D_MODEL), W_C ** -0.5)
    w_out = nrm((DEPTH, D_MODEL, D_MODEL), D_MODEL ** -0.5)
    norm_ffn_w = 1.0 + nrm((DEPTH, D_MODEL), 0.02)
    peer_wq = nrm((DEPTH, D_MODEL, PEER_HEADS * PEER_DK), D_MODEL ** -0.5)
    peer_keys = nrm((DEPTH, PEER_HEADS, 2, N_KEYS, PEER_DK_HALF), PEER_DK_HALF ** -0.5)
    peer_u = nrm((DEPTH, N_EXPERTS, D_MODEL), D_MODEL ** -0.5)
    peer_v = nrm((DEPTH, N_EXPERTS, D_MODEL), PEER_HEADS ** -0.5)
    return {'x_prompt': x_prompt, 'x_sample': x_sample, 'cache_fox_k': cache_fox_k,
            'cache_fox_v': cache_fox_v, 'cache_fox_logf': cache_fox_logf, 'page_table': page_table,
            'state_gdn': state_gdn, 'state_gdn_conv': state_gdn_conv, 'state_ssm_re': state_ssm_re,
            'state_ssm_im': state_ssm_im, 'meta_tokens': meta_tokens, 'norm_mix_w': norm_mix_w,
            'w_in': w_in, 'gdn_conv_w': gdn_conv_w, 'gdn_A_log': gdn_A_log, 'gdn_dt_bias': gdn_dt_bias,
            'gdn_norm_w': gdn_norm_w, 'fox_f_bias': fox_f_bias, 'fox_qn_w': fox_qn_w, 'fox_kn_w': fox_kn_w,
            'ssm_A_re': ssm_A_re, 'ssm_A_im': ssm_A_im, 'ssm_log_dt': ssm_log_dt, 'ssm_B_re': ssm_B_re,
            'ssm_B_im': ssm_B_im, 'ssm_C_re': ssm_C_re, 'ssm_C_im': ssm_C_im, 'ssm_D': ssm_D,
            'ssm_glu_w': ssm_glu_w, 'ssm_glu_b': ssm_glu_b, 'w_branch_a': w_branch_a,
            'w_branch_b': w_branch_b, 'w_branch_c': w_branch_c, 'w_out': w_out, 'norm_ffn_w': norm_ffn_w,
            'peer_wq': peer_wq, 'peer_keys': peer_keys, 'peer_u': peer_u, 'peer_v': peer_v}


def reference(x_prompt, x_sample, cache_fox_k, cache_fox_v, cache_fox_logf, page_table, state_gdn,
              state_gdn_conv, state_ssm_re, state_ssm_im, meta_tokens, norm_mix_w, w_in, gdn_conv_w,
              gdn_A_log, gdn_dt_bias, gdn_norm_w, fox_f_bias, fox_qn_w, fox_kn_w, ssm_A_re, ssm_A_im,
              ssm_log_dt, ssm_B_re, ssm_B_im, ssm_C_re, ssm_C_im, ssm_D, ssm_glu_w, ssm_glu_b,
              w_branch_a, w_branch_b, w_branch_c, w_out, norm_ffn_w, peer_wq, peer_keys, peer_u, peer_v):
    n_p = x_prompt.shape[0]
    n_s = x_sample.shape[0]
    meta = jnp.broadcast_to(meta_tokens.astype(x_prompt.dtype)[None], (n_p, N_META, D_MODEL))
    xp = jnp.concatenate([meta, x_prompt], axis=1)
    xs = x_sample
    prompt_rows = []
    sample_rows = []
    for l in range(DEPTH):
        lp = {'norm_mix_w': norm_mix_w[l], 'w_in': w_in[l], 'gdn_conv_w': gdn_conv_w[l],
              'gdn_A_log': gdn_A_log[l], 'gdn_dt_bias': gdn_dt_bias[l], 'gdn_norm_w': gdn_norm_w[l],
              'fox_f_bias': fox_f_bias[l], 'fox_qn_w': fox_qn_w[l], 'fox_kn_w': fox_kn_w[l],
              'ssm_A_re': ssm_A_re[l], 'ssm_A_im': ssm_A_im[l], 'ssm_log_dt': ssm_log_dt[l],
              'ssm_B_re': ssm_B_re[l], 'ssm_B_im': ssm_B_im[l], 'ssm_C_re': ssm_C_re[l],
              'ssm_C_im': ssm_C_im[l], 'ssm_D': ssm_D[l], 'ssm_glu_w': ssm_glu_w[l],
              'ssm_glu_b': ssm_glu_b[l], 'w_branch_a': w_branch_a[l], 'w_branch_b': w_branch_b[l],
              'w_branch_c': w_branch_c[l], 'w_out': w_out[l], 'norm_ffn_w': norm_ffn_w[l],
              'peer_wq': peer_wq[l], 'peer_keys': peer_keys[l], 'peer_u': peer_u[l], 'peer_v': peer_v[l]}
        xp, rows_p = hybrid_layer(xp, lp, None)
        st = {'fox_k': cache_fox_k[l][page_table].reshape(n_s, -1, H_B, DH_B),
              'fox_v': cache_fox_v[l][page_table].reshape(n_s, -1, H_B, DH_B),
              'fox_logf': cache_fox_logf[l][page_table].reshape(n_s, -1, H_B),
              'gdn_S': state_gdn[l], 'gdn_conv': state_gdn_conv[l],
              'ssm_re': state_ssm_re[l], 'ssm_im': state_ssm_im[l]}
        xs, rows_s = hybrid_layer(xs, lp, st)
        prompt_rows.append(rows_p)
        sample_rows.append(rows_s)
    y_prompt = xp[:, N_META:]
    y_sample = xs
    fox_k_prompt = jnp.stack([r[0] for r in prompt_rows])
    fox_v_prompt = jnp.stack([r[1] for r in prompt_rows])
    fox_logf_prompt = jnp.stack([r[2] for r in prompt_rows])
    gdn_state_prompt = jnp.stack([r[3] for r in prompt_rows])
    gdn_conv_prompt = jnp.stack([r[4] for r in prompt_rows])
    ssm_re_prompt = jnp.stack([r[5] for r in prompt_rows])
    ssm_im_prompt = jnp.stack([r[6] for r in prompt_rows])
    fox_k_sample = jnp.stack([r[0] for r in sample_rows])
    fox_v_sample = jnp.stack([r[1] for r in sample_rows])
    fox_logf_sample = jnp.stack([r[2] for r in sample_rows])
    gdn_state_sample = jnp.stack([r[3] for r in sample_rows])
    gdn_conv_sample = jnp.stack([r[4] for r in sample_rows])
    ssm_re_sample = jnp.stack([r[5] for r in sample_rows])
    ssm_im_sample = jnp.stack([r[6] for r in sample_rows])
    return (y_prompt, y_sample, fox_k_prompt, fox_v_prompt, fox_logf_prompt, gdn_state_prompt,
            gdn_conv_prompt, ssm_re_prompt, ssm_im_prompt, fox_k_sample, fox_v_sample, fox_logf_sample,
            gdn_state_sample, gdn_conv_sample, ssm_re_sample, ssm_im_sample)
```

```python
import functools
import math

import jax
import jax.numpy as jnp
from jax import lax
from jax.experimental import pallas as pl
from jax.experimental.pallas import tpu as pltpu

F32 = jnp.float32
BF16 = jnp.bfloat16
HIGHEST = lax.Precision.HIGHEST

D_MODEL = 1024
N_META = 16
ROW_BLOCK = 128
PAD_ROWS = ROW_BLOCK - N_META
ROW_TILE = 512
H_A, DK_A, DV_A = 4, 128, 128
W_A = H_A * DV_A
QKV_A = 2 * H_A * DK_A + H_A * DV_A
CONV_W = 4
H_B, DH_B = 8, 64
W_B = H_B * DH_B
HC, G_C, P_C = 16, 32, 64
W_C = G_C * HC
N_STATE_C = G_C * P_C
PEER_HEADS, PEER_TOPK, PEER_DK_HALF = 8, 16, 64
PEER_SLOTS = PEER_HEADS * PEER_TOPK
EPS = 1e-6
NEG_BIG = -1e30
VMEM_LIMIT = 56 * 1024 * 1024

Z_QKV, Z_ZA, Z_QB, Z_KB, Z_VB, Z_UC, Z_GA, Z_GB, Z_GC, Z_SM = (
    0, 1536, 2048, 2560, 3072, 3584, 4096, 5120, 6144, 7168)
Z_WIDTH = 7680
SM_F, SM_A, SM_B = 0, 8, 12


def _cparams(sem):
    return pltpu.CompilerParams(dimension_semantics=sem, vmem_limit_bytes=VMEM_LIMIT)


def _mm(a, b):
    return jnp.dot(a.astype(BF16), b.astype(BF16), preferred_element_type=F32)


def _mm_nt(a, b):
    return lax.dot_general(a.astype(BF16), b.astype(BF16), (((1,), (1,)), ((), ())),
                           preferred_element_type=F32)


def _mm_tn(a, b):
    return lax.dot_general(a.astype(BF16), b.astype(BF16), (((0,), (0,)), ((), ())),
                           preferred_element_type=F32)


def _mm32(a, b):
    return jnp.dot(a, b, precision=HIGHEST, preferred_element_type=F32)


def _sigmoid(x):
    return 1.0 / (1.0 + jnp.exp(-x))


def _softplus(x):
    return jnp.maximum(x, 0.0) + jnp.log(1.0 + jnp.exp(-jnp.abs(x)))


def _gelu(x):
    c = math.sqrt(2.0 / math.pi)
    return 0.5 * x * (1.0 + jnp.tanh(c * (x + 0.044715 * (x * x * x))))


def _iota(shape, dim):
    return lax.broadcasted_iota(jnp.int32, shape, dim)


def _inproj_kernel(x_ref, mask_ref, nw_ref, w_ref, z_ref, h_scr):
    @pl.when(pl.program_id(1) == 0)
    def _():
        x = x_ref[...]
        ms = jnp.mean(x * x, axis=-1, keepdims=True)
        h = x * lax.rsqrt(ms + EPS) * nw_ref[...] * mask_ref[...]
        h_scr[...] = h.astype(BF16)

    z_ref[...] = jnp.dot(h_scr[...], w_ref[...], preferred_element_type=F32)


def _inproj(x, mask, nw, w):
    nr = x.shape[0]
    tn = 512
    return pl.pallas_call(
        _inproj_kernel,
        grid=(nr // ROW_TILE, Z_WIDTH // tn),
        in_specs=[pl.BlockSpec((ROW_TILE, D_MODEL), lambda i, j: (i, 0)),
                  pl.BlockSpec((ROW_TILE, 1), lambda i, j: (i, 0)),
                  pl.BlockSpec((1, D_MODEL), lambda i, j: (0, 0)),
                  pl.BlockSpec((D_MODEL, tn), lambda i, j: (0, j))],
        out_specs=pl.BlockSpec((ROW_TILE, tn), lambda i, j: (i, j)),
        out_shape=jax.ShapeDtypeStruct((nr, Z_WIDTH), F32),
        scratch_shapes=[pltpu.VMEM((ROW_TILE, D_MODEL), BF16)],
        compiler_params=_cparams(("parallel", "arbitrary")),
        name="inproj",
    )(x, mask, nw, w)


def _unit_lower_inverse(a_mat, eye):
    n = a_mat.shape[0]
    p = -a_mat
    t = eye + p
    size = 2
    while size < n:
        p = _mm32(p, p)
        t = t + _mm32(t, p)
        size *= 2
    return t


def _gdn_prompt_kernel(qkv_ref, za_ref, sm_ref, cw_ref, alog_ref, dtb_ref, acol_ref, dcol_ref, nw_ref,
                       o_ref, s_out_ref, s_scr, tail_scr):
    i = pl.program_id(1)
    c = ROW_BLOCK

    @pl.when(i == 0)
    def _():
        s_scr[...] = jnp.zeros_like(s_scr)
        tail_scr[...] = jnp.zeros_like(tail_scr)

    x = qkv_ref[...]
    tl = tail_scr[...]
    row8 = _iota((8, QKV_A), 0)
    y = cw_ref[CONV_W - 1:CONV_W, :] * x
    for j in range(1, CONV_W):
        xs = pltpu.roll(x, j, 0)
        ts = pltpu.roll(tl, j, 0)
        head = jnp.where(row8 < j, ts, xs[:8])
        xs = jnp.concatenate([head, xs[8:]], axis=0)
        y = y + cw_ref[CONV_W - 1 - j:CONV_W - j, :] * xs
    tail_scr[...] = x[c - 8:]
    y = y * _sigmoid(y)

    sm = sm_ref[...]
    g_all = -jnp.exp(alog_ref[...]) * _softplus(sm + dtb_ref[...])
    beta_all = _sigmoid(sm)
    rr = _iota((c, c), 0)
    cc = _iota((c, c), 1)
    incl = rr >= cc
    strict = rr > cc
    eye = jnp.where(rr == cc, 1.0, 0.0).astype(F32)
    lower = jnp.where(incl, 1.0, 0.0).astype(F32)
    upper = jnp.where(rr <= cc, 1.0, 0.0).astype(F32)
    gc_col_all = _mm32(lower, g_all)
    smt = sm.T
    slab = smt[SM_A:SM_A + 8, :]
    g_row = -jnp.exp(acol_ref[...]) * _softplus(slab + dcol_ref[...])
    gc_row_all = _mm32(g_row, upper)

    za = za_ref[...]
    outs = []
    for h in range(H_A):
        q = y[:, h * DK_A:(h + 1) * DK_A]
        k = y[:, (H_A + h) * DK_A:(H_A + h + 1) * DK_A]
        v = y[:, 2 * H_A * DK_A + h * DV_A:2 * H_A * DK_A + (h + 1) * DV_A]
        q = q * lax.rsqrt(jnp.sum(q * q, axis=-1, keepdims=True) + EPS) * (DK_A ** -0.5)
        k = k * lax.rsqrt(jnp.sum(k * k, axis=-1, keepdims=True) + EPS)
        gc = gc_col_all[:, SM_A + h:SM_A + h + 1]
        gcr = gc_row_all[h:h + 1, :]
        beta = beta_all[:, SM_B + h:SM_B + h + 1]
        glast = gc[c - 1:c, :]
        eg = jnp.exp(gc)
        decay = jnp.where(incl, jnp.exp(jnp.where(incl, gc - gcr, 0.0)), 0.0)
        kb = k * beta
        a_mat = jnp.where(strict, _mm_nt(kb, k) * decay, 0.0)
        t_inv = _unit_lower_inverse(a_mat, eye)
        u = _mm(t_inv, v * beta)
        w = _mm(t_inv, kb * eg)
        s_h = s_scr[h]
        v_new = u - _mm(w, s_h)
        attn = _mm_nt(q, k) * decay
        o = _mm(q * eg, s_h) + _mm(attn, v_new)
        k_dec = k * jnp.exp(glast - gc)
        s_scr[h] = s_h * jnp.exp(glast) + _mm_tn(k_dec, v_new)
        o = o * lax.rsqrt(jnp.mean(o * o, axis=-1, keepdims=True) + EPS) * nw_ref[...]
        zh = za[:, h * DV_A:(h + 1) * DV_A]
        outs.append(o * (zh * _sigmoid(zh)))
    o_ref[...] = jnp.concatenate(outs, axis=1).astype(o_ref.dtype)

    @pl.when(i == pl.num_programs(1) - 1)
    def _():
        s_out_ref[0] = s_scr[...]


def _gdn_prompt(z, cw, alog_row, dtb_row, alog_col, dtb_col, nw, n_batch, nblk):
    c = ROW_BLOCK
    return pl.pallas_call(
        _gdn_prompt_kernel,
        grid=(n_batch, nblk),
        in_specs=[pl.BlockSpec((c, QKV_A), lambda b, i: (b * nblk + i, Z_QKV // QKV_A)),
                  pl.BlockSpec((c, W_A), lambda b, i: (b * nblk + i, Z_ZA // W_A)),
                  pl.BlockSpec((c, 128), lambda b, i: (b * nblk + i, Z_SM // 128)),
                  pl.BlockSpec((CONV_W, QKV_A), lambda b, i: (0, 0)),
                  pl.BlockSpec((1, 128), lambda b, i: (0, 0)),
                  pl.BlockSpec((1, 128), lambda b, i: (0, 0)),
                  pl.BlockSpec((8, 1), lambda b, i: (0, 0)),
                  pl.BlockSpec((8, 1), lambda b, i: (0, 0)),
                  pl.BlockSpec((1, DV_A), lambda b, i: (0, 0))],
        out_specs=[pl.BlockSpec((c, W_A), lambda b, i: (b * nblk + i, 0)),
                   pl.BlockSpec((1, H_A, DK_A, DV_A), lambda b, i: (b, 0, 0, 0))],
        out_shape=[jax.ShapeDtypeStruct((n_batch * nblk * c, W_A), BF16),
                   jax.ShapeDtypeStruct((n_batch, H_A, DK_A, DV_A), F32)],
        scratch_shapes=[pltpu.VMEM((H_A, DK_A, DV_A), F32), pltpu.VMEM((8, QKV_A), F32)],
        compiler_params=_cparams(("parallel", "arbitrary")),
        name="gdn_prompt",
    )(z, z, z, cw, alog_row, dtb_row, alog_col, dtb_col, nw)


def _gdn_sample_kernel(layer, qkv_ref, za_ref, sm_ref, conv_ref, s_ref, cw_ref, alog_ref,
                       dtb_ref, nw_ref, o_ref, s_out_ref):
    del layer
    b = pl.program_id(0)
    x = qkv_ref[pl.ds(b, 1), :]
    buf = conv_ref[0, 0]
    y = cw_ref[CONV_W - 1:CONV_W, :] * x
    for j in range(CONV_W - 1):
        y = y + cw_ref[j:j + 1, :] * buf[j:j + 1, :]
    y = y * _sigmoid(y)
    sm = sm_ref[pl.ds(b, 1), :]
    g_all = -jnp.exp(alog_ref[...]) * _softplus(sm + dtb_ref[...])
    beta_all = _sigmoid(sm)
    za = za_ref[pl.ds(b, 1), :]
    outs = []
    for h in range(H_A):
        q = y[:, h * DK_A:(h + 1) * DK_A]
        k = y[:, (H_A + h) * DK_A:(H_A + h + 1) * DK_A]
        v = y[:, 2 * H_A * DK_A + h * DV_A:2 * H_A * DK_A + (h + 1) * DV_A]
        q = q * lax.rsqrt(jnp.sum(q * q, axis=-1, keepdims=True) + EPS) * (DK_A ** -0.5)
        k = k * lax.rsqrt(jnp.sum(k * k, axis=-1, keepdims=True) + EPS)
        g = g_all[:, SM_A + h:SM_A + h + 1]
        beta = beta_all[:, SM_B + h:SM_B + h + 1]
        eg = jnp.exp(g)
        s_h = s_ref[0, 0, h]
        kb = k * beta
        w8 = jnp.broadcast_to(kb * eg, (8, DK_A))
        q8 = jnp.broadcast_to(q * eg, (8, DK_A))
        v_new = v * beta - _mm32(w8, s_h)[0:1, :]
        attn = jnp.sum(q * k, axis=-1, keepdims=True)
        o = _mm32(q8, s_h)[0:1, :] + attn * v_new
        k_cols = jnp.broadcast_to(k, (DK_A, DK_A)).T
        s_out_ref[0, h] = s_h * eg + k_cols * v_new
        o = o * lax.rsqrt(jnp.mean(o * o, axis=-1, keepdims=True) + EPS) * nw_ref[...]
        zh = za[:, h * DV_A:(h + 1) * DV_A]
        outs.append(o * (zh * _sigmoid(zh)))
    o_ref[pl.ds(b, 1), :] = jnp.concatenate(outs, axis=1)


def _gdn_sample(layer, qkv_s, za_s, sm_s, state_conv, state_s, cw, alog_row, dtb_row, nw):
    nb = qkv_s.shape[0]
    full = lambda shape: pl.BlockSpec(shape, lambda b: (0,) * len(shape))
    return pl.pallas_call(
        functools.partial(_gdn_sample_kernel, layer),
        grid=(nb,),
        in_specs=[full((nb, QKV_A)), full((nb, W_A)), full((nb, 128)),
                  pl.BlockSpec((1, 1, CONV_W - 1, QKV_A), lambda b: (layer, b, 0, 0)),
                  pl.BlockSpec((1, 1, H_A, DK_A, DV_A), lambda b: (layer, b, 0, 0, 0)),
                  full((CONV_W, QKV_A)), full((1, 128)), full((1, 128)), full((1, DV_A))],
        out_specs=[full((nb, W_A)),
                   pl.BlockSpec((1, H_A, DK_A, DV_A), lambda b: (b, 0, 0, 0))],
        out_shape=[jax.ShapeDtypeStruct((nb, W_A), F32),
                   jax.ShapeDtypeStruct((nb, H_A, DK_A, DV_A), F32)],
        compiler_params=_cparams(("arbitrary",)),
        name="gdn_sample",
    )(qkv_s, za_s, sm_s, state_conv, state_s, cw, alog_row, dtb_row, nw)


def _fox_prep_kernel(nblk, q_ref, k_ref, v_ref, sm_ref, mask_ref, qw_ref, kw_ref, fb_ref, grp_ref,
                     qn_ref, kn_ref, knb_ref, vb_ref, logf_ref, ccol_ref, crow_ref, carry_scr):
    i = pl.program_id(0)
    c = ROW_BLOCK

    @pl.when(i % nblk == 0)
    def _():
        carry_scr[...] = jnp.zeros_like(carry_scr)

    grp = grp_ref[...]
    q = q_ref[...]
    k = k_ref[...]
    qn = q * lax.rsqrt(_mm32(q * q, grp) * (1.0 / DH_B) + EPS) * qw_ref[...]
    kn = k * lax.rsqrt(_mm32(k * k, grp) * (1.0 / DH_B) + EPS) * kw_ref[...]
    qn_ref[...] = (qn * (DH_B ** -0.5)).astype(BF16)
    kn_ref[...] = kn
    knb_ref[...] = kn.astype(BF16)
    vb_ref[...] = v_ref[...].astype(BF16)

    mask = mask_ref[...]
    xf = sm_ref[...] + fb_ref[...]
    logf = (jnp.minimum(xf, 0.0) - jnp.log(1.0 + jnp.exp(-jnp.abs(xf)))) * mask
    logf_ref[...] = logf[:, SM_F:SM_F + H_B]
    rr = _iota((c, c), 0)
    cc = _iota((c, c), 1)
    lower = jnp.where(rr >= cc, 1.0, 0.0).astype(F32)
    cum = _mm32(lower, logf) + carry_scr[...]
    carry_scr[...] = cum[c - 1:c, :]
    ccol_ref[...] = cum[:, SM_F:SM_F + H_B]
    crow_ref[0] = jnp.where(mask > 0.0, cum, -NEG_BIG).T[SM_F:SM_F + H_B, :]


def _fox_prep(z, mask, qw, kw, fb_row, grp, nblk):
    nr = z.shape[0]
    c = ROW_BLOCK
    row = lambda col: pl.BlockSpec((c, W_B), lambda i, col=col: (i, col))
    return pl.pallas_call(
        functools.partial(_fox_prep_kernel, nblk),
        grid=(nr // c,),
        in_specs=[row(Z_QB // W_B), row(Z_KB // W_B), row(Z_VB // W_B),
                  pl.BlockSpec((c, 128), lambda i: (i, Z_SM // 128)),
                  pl.BlockSpec((c, 1), lambda i: (i, 0)),
                  pl.BlockSpec((1, W_B), lambda i: (0, 0)),
                  pl.BlockSpec((1, W_B), lambda i: (0, 0)),
                  pl.BlockSpec((1, 128), lambda i: (0, 0)),
                  pl.BlockSpec((W_B, W_B), lambda i: (0, 0))],
        out_specs=[row(0), row(0), row(0), row(0),
                   pl.BlockSpec((c, H_B), lambda i: (i, 0)),
                   pl.BlockSpec((c, H_B), lambda i: (i, 0)),
                   pl.BlockSpec((1, H_B, c), lambda i: (i, 0, 0))],
        out_shape=[jax.ShapeDtypeStruct((nr, W_B), BF16),
                   jax.ShapeDtypeStruct((nr, W_B), F32),
                   jax.ShapeDtypeStruct((nr, W_B), BF16),
                   jax.ShapeDtypeStruct((nr, W_B), BF16),
                   jax.ShapeDtypeStruct((nr, H_B), F32),
                   jax.ShapeDtypeStruct((nr, H_B), F32),
                   jax.ShapeDtypeStruct((nr // c, H_B, c), F32)],
        scratch_shapes=[pltpu.VMEM((1, 128), F32)],
        compiler_params=_cparams(("arbitrary",)),
        name="fox_prep",
    )(z, z, z, z, mask, qw, kw, fb_row, grp)


def _fox_prompt_kernel(q_ref, k_ref, v_ref, cc_ref, cr_ref, o_ref):
    i = pl.program_id(2)
    c = ROW_BLOCK
    q = q_ref[...]
    lane = _iota((c, 128), 1)
    first = lane < DH_B
    zero = jnp.zeros_like(q)
    qs = (jnp.where(first, q, zero), jnp.where(first, zero, q))
    cq = (cc_ref[0, :, 0:1], cc_ref[0, :, 1:2])
    rr = _iota((c, c), 0)
    kk = _iota((c, c), 1)
    causal = kk <= rr

    def block(j, carry, diag):
        m, l, acc = carry
        kb = k_ref[pl.ds(pl.multiple_of(j * c, c), c), :]
        vb = v_ref[pl.ds(pl.multiple_of(j * c, c), c), :]
        cr = cr_ref[0, j]
        m_new, l_new, pv, alpha = [], [], [], []
        for h in range(2):
            s = lax.dot_general(qs[h], kb, (((1,), (1,)), ((), ())), preferred_element_type=F32)
            s = s + (cq[h] - cr[h:h + 1, :])
            if diag:
                s = jnp.where(causal, s, NEG_BIG)
            mh = jnp.maximum(m[h], jnp.max(s, axis=-1, keepdims=True))
            p = jnp.exp(s - mh)
            a = jnp.exp(m[h] - mh)
            l_new.append(a * l[h] + jnp.sum(p, axis=-1, keepdims=True))
            m_new.append(mh)
            alpha.append(a)
            pv.append(jnp.dot(p.astype(BF16), vb, preferred_element_type=F32))
        acc = jnp.where(first, alpha[0] * acc + pv[0], alpha[1] * acc + pv[1])
        return tuple(m_new), tuple(l_new), acc

    init_m = jnp.full((c, 1), NEG_BIG, F32)
    init_l = jnp.zeros((c, 1), F32)
    carry = ((init_m, init_m), (init_l, init_l), jnp.zeros((c, 128), F32))
    carry = lax.fori_loop(0, i, lambda j, cr_: block(j, cr_, False), carry)
    m, l, acc = block(i, carry, True)
    o_ref[...] = jnp.where(first, acc / l[0], acc / l[1]).astype(o_ref.dtype)


def _fox_prompt(qn, knb, vb, ccol_p, crow_p, n_batch, nblk):
    c = ROW_BLOCK
    tp = nblk * c
    n_pair = H_B // 2
    return pl.pallas_call(
        _fox_prompt_kernel,
        grid=(n_batch, n_pair, nblk),
        in_specs=[pl.BlockSpec((c, 128), lambda b, p, i: (b * nblk + i, p)),
                  pl.BlockSpec((tp, 128), lambda b, p, i: (b, p)),
                  pl.BlockSpec((tp, 128), lambda b, p, i: (b, p)),
                  pl.BlockSpec((1, c, 2), lambda b, p, i: (p, b * nblk + i, 0)),
                  pl.BlockSpec((1, nblk, 2, c), lambda b, p, i: (p, b, 0, 0))],
        out_specs=pl.BlockSpec((c, 128), lambda b, p, i: (b * nblk + i, p)),
        out_shape=jax.ShapeDtypeStruct((n_batch * tp, W_B), BF16),
        compiler_params=_cparams(("parallel", "parallel", "arbitrary")),
        name="fox_prompt",
    )(qn, knb, vb, ccol_p, crow_p)


def _fox_sample_kernel(n_pages, pt_ref, qs_ref, ks_ref, vs_ref, lfs_ref, kc_ref, vc_ref, lc_ref,
                       o_ref, q_scr, m_scr, l_scr, r_scr, acc_scr):
    del pt_ref
    b = pl.program_id(0)
    j = pl.program_id(1)
    c = ROW_BLOCK
    hsel = lax.shift_right_logical(_iota((H_B, W_B), 1), DH_B.bit_length() - 1) == _iota(
        (H_B, W_B), 0)

    @pl.when(j == 0)
    def _():
        qrow = qs_ref[pl.ds(b, 1), :]
        qblk = jnp.where(hsel, jnp.broadcast_to(qrow, (H_B, W_B)), 0.0)
        q_scr[...] = qblk
        m_scr[...] = jnp.sum(qblk * ks_ref[pl.ds(b, 1), :], axis=-1, keepdims=True)
        l_scr[...] = jnp.ones_like(l_scr)
        acc_scr[...] = jnp.broadcast_to(vs_ref[pl.ds(b, 1), :], (H_B, W_B))
        lrow = jnp.broadcast_to(lfs_ref[pl.ds(b, 1), :], (H_B, 128))
        r_scr[...] = jnp.sum(jnp.where(_iota((H_B, 128), 1) == _iota((H_B, 128), 0), lrow, 0.0),
                             axis=-1, keepdims=True)

    kpage = kc_ref[0, 0]
    vpage = vc_ref[0, 0]
    lt = lc_ref[0, 0]
    later = jnp.where(_iota((c, c), 0) > _iota((c, c), 1), 1.0, 0.0).astype(F32)
    bias = r_scr[...] + _mm32(lt, later)
    r_scr[...] = r_scr[...] + jnp.sum(lt, axis=-1, keepdims=True)
    s = _mm_nt(q_scr[...], kpage) + bias
    m_old = m_scr[...]
    m_new = jnp.maximum(m_old, jnp.max(s, axis=-1, keepdims=True))
    p = jnp.exp(s - m_new)
    alpha = jnp.exp(m_old - m_new)
    l_scr[...] = alpha * l_scr[...] + jnp.sum(p, axis=-1, keepdims=True)
    m_scr[...] = m_new
    acc_scr[...] = alpha * acc_scr[...] + _mm(p, vpage)

    @pl.when(j == n_pages - 1)
    def _():
        o = jnp.where(hsel, acc_scr[...] / l_scr[...], 0.0)
        o_ref[pl.ds(b, 1), :] = jnp.sum(o, axis=0, keepdims=True)


def _fox_sample(layer, page_table, qs, ks, vs, lfs, cache_k, cache_v, cache_lt):
    nb, n_pages = page_table.shape
    c = cache_k.shape[2]
    full = lambda shape: pl.BlockSpec(shape, lambda b, j, pt: (0,) * len(shape))
    page = lambda b, j, pt: (layer, pt[b * n_pages + (n_pages - 1 - j)], 0, 0)
    grid_spec = pltpu.PrefetchScalarGridSpec(
        num_scalar_prefetch=1,
        grid=(nb, n_pages),
        in_specs=[full((nb, W_B)), full((nb, W_B)), full((nb, W_B)), full((nb, 128)),
                  pl.BlockSpec((1, 1, c, W_B), page),
                  pl.BlockSpec((1, 1, c, W_B), page),
                  pl.BlockSpec((1, 1, H_B, c), page)],
        out_specs=full((nb, W_B)),
        scratch_shapes=[pltpu.VMEM((H_B, W_B), F32), pltpu.VMEM((H_B, 1), F32),
                        pltpu.VMEM((H_B, 1), F32), pltpu.VMEM((H_B, 1), F32),
                        pltpu.VMEM((H_B, W_B), F32)])
    return pl.pallas_call(
        functools.partial(_fox_sample_kernel, n_pages),
        grid_spec=grid_spec,
        out_shape=jax.ShapeDtypeStruct((nb, W_B), F32),
        compiler_params=_cparams(("arbitrary", "arbitrary")),
        name="fox_sample",
    )(page_table.reshape(-1), qs, ks, vs, lfs, cache_k, cache_v, cache_lt)


def _s5_readout(u, h_re, h_im, cre_ref, cim_ref, d_ref, gw_ref, gb_ref):
    y = _mm(h_re, cre_ref[...]) - _mm(h_im, cim_ref[...]) + d_ref[...] * u
    y = _gelu(y)
    return y * _sigmoid(_mm(y, gw_ref[...]) + gb_ref[...])


def _s5_prompt_kernel(u_ref, bre_ref, bim_ref, lre_ref, lim_ref, cre_ref, cim_ref, d_ref, gw_ref,
                      gb_ref, o_ref, hre_out_ref, him_out_ref, hre_scr, him_scr, cre_scr, cim_scr):
    i = pl.program_id(1)
    c = ROW_BLOCK

    @pl.when(i == 0)
    def _():
        cre_scr[...] = jnp.zeros_like(cre_scr)
        cim_scr[...] = jnp.zeros_like(cim_scr)

    u = u_ref[...]
    hre_scr[...] = _mm(u, bre_ref[...])
    him_scr[...] = _mm(u, bim_ref[...])
    lre = lre_ref[...]
    lim = lim_ref[...]

    def step(t, carry):
        hr, hi = carry
        nr = lre * hr - lim * hi + hre_scr[pl.ds(t, 1), :]
        ni = lre * hi + lim * hr + him_scr[pl.ds(t, 1), :]
        hre_scr[pl.ds(t, 1), :] = nr
        him_scr[pl.ds(t, 1), :] = ni
        return nr, ni

    hr, hi = lax.fori_loop(0, c, step, (cre_scr[...], cim_scr[...]), unroll=8)
    cre_scr[...] = hr
    cim_scr[...] = hi
    o_ref[...] = _s5_readout(u, hre_scr[...], him_scr[...], cre_ref, cim_ref, d_ref, gw_ref,
                             gb_ref).astype(o_ref.dtype)

    @pl.when(i == pl.num_programs(1) - 1)
    def _():
        hre_out_ref[0] = hr
        him_out_ref[0] = hi


def _s5_weight_specs(index_map):
    return [pl.BlockSpec((W_C, N_STATE_C), index_map), pl.BlockSpec((W_C, N_STATE_C), index_map),
            pl.BlockSpec((1, N_STATE_C), index_map), pl.BlockSpec((1, N_STATE_C), index_map),
            pl.BlockSpec((N_STATE_C, W_C), index_map), pl.BlockSpec((N_STATE_C, W_C), index_map),
            pl.BlockSpec((1, W_C), index_map), pl.BlockSpec((W_C, W_C), index_map),
            pl.BlockSpec((1, W_C), index_map)]


def _s5_prompt(z, s5w, n_batch, nblk):
    c = ROW_BLOCK
    return pl.pallas_call(
        _s5_prompt_kernel,
        grid=(n_batch, nblk),
        in_specs=[pl.BlockSpec((c, W_C), lambda b, i: (b * nblk + i, Z_UC // W_C))]
        + _s5_weight_specs(lambda b, i: (0, 0)),
        out_specs=[pl.BlockSpec((c, W_C), lambda b, i: (b * nblk + i, 0)),
                   pl.BlockSpec((1, 1, N_STATE_C), lambda b, i: (b, 0, 0)),
                   pl.BlockSpec((1, 1, N_STATE_C), lambda b, i: (b, 0, 0))],
        out_shape=[jax.ShapeDtypeStruct((n_batch * nblk * c, W_C), BF16),
                   jax.ShapeDtypeStruct((n_batch, 1, N_STATE_C), F32),
                   jax.ShapeDtypeStruct((n_batch, 1, N_STATE_C), F32)],
        scratch_shapes=[pltpu.VMEM((c, N_STATE_C), F32), pltpu.VMEM((c, N_STATE_C), F32),
                        pltpu.VMEM((1, N_STATE_C), F32), pltpu.VMEM((1, N_STATE_C), F32)],
        compiler_params=_cparams(("parallel", "arbitrary")),
        name="s5_prompt",
    )(z, *s5w)


def _s5_sample_kernel(u_ref, h0re_ref, h0im_ref, bre_ref, bim_ref, lre_ref, lim_ref, cre_ref,
                      cim_ref, d_ref, gw_ref, gb_ref, o_ref, hre_ref, him_ref):
    u = u_ref[...]
    lre = lre_ref[...]
    lim = lim_ref[...]
    r0 = h0re_ref[...]
    i0 = h0im_ref[...]
    h_re = _mm(u, bre_ref[...]) + lre * r0 - lim * i0
    h_im = _mm(u, bim_ref[...]) + lre * i0 + lim * r0
    hre_ref[...] = h_re
    him_ref[...] = h_im
    o_ref[...] = _s5_readout(u, h_re, h_im, cre_ref, cim_ref, d_ref, gw_ref, gb_ref)


def _s5_sample(u_s, h0_re, h0_im, s5w):
    nb = u_s.shape[0]
    return pl.pallas_call(
        _s5_sample_kernel,
        out_shape=[jax.ShapeDtypeStruct((nb, W_C), F32),
                   jax.ShapeDtypeStruct((nb, N_STATE_C), F32),
                   jax.ShapeDtypeStruct((nb, N_STATE_C), F32)],
        compiler_params=pltpu.CompilerParams(vmem_limit_bytes=VMEM_LIMIT),
        name="s5_sample",
    )(u_s, h0_re, h0_im, *s5w)


def _merge_kernel(x_ref, oa_ref, ob_ref, oc_ref, ga_ref, gb_ref, gc_ref, wa_ref, wb_ref, wc_ref,
                  wo_ref, y_ref):
    mix = (_sigmoid(ga_ref[...]) * _mm(oa_ref[...], wa_ref[...])
           + _sigmoid(gb_ref[...]) * _mm(ob_ref[...], wb_ref[...])
           + _sigmoid(gc_ref[...]) * _mm(oc_ref[...], wc_ref[...]))
    y_ref[...] = x_ref[...] + _mm(mix, wo_ref[...])


def _merge(x, oa, ob, oc, z, wa, wb, wc, wo):
    nr = x.shape[0]
    t = ROW_TILE
    row = lambda w, col=0: pl.BlockSpec((t, w), lambda i, col=col: (i, col))
    const = lambda r, w: pl.BlockSpec((r, w), lambda i: (0, 0))
    return pl.pallas_call(
        _merge_kernel,
        grid=(nr // t,),
        in_specs=[row(D_MODEL), row(W_A), row(W_B), row(W_C),
                  row(D_MODEL, Z_GA // D_MODEL), row(D_MODEL, Z_GB // D_MODEL),
                  row(D_MODEL, Z_GC // D_MODEL),
                  const(W_A, D_MODEL), const(W_B, D_MODEL), const(W_C, D_MODEL),
                  const(D_MODEL, D_MODEL)],
        out_specs=row(D_MODEL),
        out_shape=jax.ShapeDtypeStruct((nr, D_MODEL), F32),
        compiler_params=_cparams(("parallel",)),
        name="merge",
    )(x, oa, ob, oc, z, z, z, wa, wb, wc, wo)


PEER_TM = 256
PEER_EC = 2048


def _top16(x, iota_f):
    n_rows = float(x.shape[0])
    vals, idxs = [], []
    for _ in range(PEER_TOPK):
        m = jnp.max(x, axis=0, keepdims=True)
        am = jnp.min(jnp.where(x == m, iota_f, n_rows), axis=0, keepdims=True)
        x = jnp.where(iota_f == am, -jnp.inf, x)
        vals.append(m)
        idxs.append(am)
    return jnp.concatenate(vals, axis=0), jnp.concatenate(idxs, axis=0)


def _peer_route_kernel(n_keys, x_ref, nw_ref, wq_ref, kb_ref, h_ref, a_ref, b_ref, g_ref, s_scr):
    x = x_ref[...]
    ms = jnp.mean(x * x, axis=-1, keepdims=True)
    h = (x * lax.rsqrt(ms + EPS) * nw_ref[...]).astype(BF16)
    h_ref[...] = h
    q = jnp.dot(h, wq_ref[...], preferred_element_type=F32).astype(BF16)
    s_scr[...] = lax.dot_general(kb_ref[...], q, (((1,), (1,)), ((), ())),
                                 preferred_element_type=F32)
    iota_k = _iota((n_keys, 128), 0).astype(F32)
    iota_c = _iota((PEER_TOPK * PEER_TOPK, 128), 0).astype(F32)
    iota_r = _iota((PEER_TOPK, 128), 0).astype(F32)

    def head_body(hd, carry):
        for grp in range(PEER_TM // 128):
            lanes = slice(grp * 128, (grp + 1) * 128)
            base = pl.multiple_of(hd * (2 * n_keys), 2 * n_keys)
            s1, i1 = _top16(s_scr[pl.ds(base, n_keys), lanes], iota_k)
            s2, i2 = _top16(s_scr[pl.ds(base + n_keys, n_keys), lanes], iota_k)
            cand = jnp.concatenate([s1[r:r + 1, :] + s2 for r in range(PEER_TOPK)], axis=0)
            top_s, pos = _top16(cand, iota_c)
            pi = jnp.floor(pos * (1.0 / PEER_TOPK))
            pj = pos - pi * PEER_TOPK
            a_sel = jnp.zeros_like(pos)
            b_sel = jnp.zeros_like(pos)
            for r in range(PEER_TOPK):
                a_sel = a_sel + jnp.where(pi == float(r), i1[r:r + 1, :], 0.0)
                b_sel = b_sel + jnp.where(pj == float(r), i2[r:r + 1, :], 0.0)
            e = jnp.exp(top_s - top_s[0:1, :])
            gate = e / jnp.sum(e, axis=0, keepdims=True)
            rows = pl.ds(pl.multiple_of(hd * PEER_TOPK, PEER_TOPK), PEER_TOPK)
            a_ref[rows, lanes] = a_sel
            b_ref[rows, lanes] = b_sel
            g_ref[rows, lanes] = gate
        return carry

    del iota_r
    lax.fori_loop(0, PEER_HEADS, head_body, 0)


def _peer_route(x, nw, wq, kbd, n_keys):
    nr = x.shape[0]
    t = PEER_TM
    n_scores = PEER_HEADS * 2 * n_keys
    slot = pl.BlockSpec((PEER_SLOTS, t), lambda i: (0, i))
    return pl.pallas_call(
        functools.partial(_peer_route_kernel, n_keys),
        grid=(nr // t,),
        in_specs=[pl.BlockSpec((t, D_MODEL), lambda i: (i, 0)),
                  pl.BlockSpec((1, D_MODEL), lambda i: (0, 0)),
                  pl.BlockSpec((D_MODEL, D_MODEL), lambda i: (0, 0)),
                  pl.BlockSpec((n_scores, D_MODEL), lambda i: (0, 0))],
        out_specs=[pl.BlockSpec((t, D_MODEL), lambda i: (i, 0)), slot, slot, slot],
        out_shape=[jax.ShapeDtypeStruct((nr, D_MODEL), BF16),
                   jax.ShapeDtypeStruct((PEER_SLOTS, nr), F32),
                   jax.ShapeDtypeStruct((PEER_SLOTS, nr), F32),
                   jax.ShapeDtypeStruct((PEER_SLOTS, nr), F32)],
        scratch_shapes=[pltpu.VMEM((n_scores, t), F32)],
        compiler_params=_cparams(("parallel",)),
        name="peer_route",
    )(x, nw, wq, kbd)


def _peer_expert_kernel(n_keys, x_ref, h_ref, a_ref, b_ref, g_ref, ut_ref, v_ref, y_ref,
                        w_scr, a_scr, b_scr, g_scr):
    j = pl.program_id(1)
    t = PEER_TM

    @pl.when(j == 0)
    def _():
        for grp in range(t // 128):
            lanes = slice(grp * 128, (grp + 1) * 128)
            rows = slice(grp * 128, (grp + 1) * 128)
            a_scr[rows, :] = a_ref[:, lanes].T
            b_scr[rows, :] = b_ref[:, lanes].T
            g_scr[rows, :] = g_ref[:, lanes].T
        iota_s = _iota((n_keys, PEER_SLOTS), 0).astype(F32)

        def token_body(n, carry):
            ar = a_scr[pl.ds(n, 1), :]
            br = b_scr[pl.ds(n, 1), :]
            gr = g_scr[pl.ds(n, 1), :]
            oat = jnp.where(iota_s == ar, 1.0, 0.0).astype(BF16)
            wbt = jnp.where(iota_s == br, gr, 0.0).astype(BF16)
            wn = lax.dot_general(oat, wbt, (((1,), (1,)), ((), ())), preferred_element_type=F32)
            w_scr[pl.ds(pl.multiple_of(n * n_keys, n_keys), n_keys), :] = wn
            return carry

        lax.fori_loop(0, t, token_body, 0, unroll=4)
        y_ref[...] = x_ref[...]

    act = jnp.dot(h_ref[...], ut_ref[...], preferred_element_type=F32)
    n_a = PEER_EC // n_keys
    parts = []
    for al in range(n_a):
        wa = w_scr[pl.ds(j * n_a + al, t, stride=n_keys), :]
        parts.append((wa * _gelu(act[:, al * n_keys:(al + 1) * n_keys])).astype(BF16))
    p = jnp.concatenate(parts, axis=1)
    y_ref[...] += jnp.dot(p, v_ref[...], preferred_element_type=F32)


def _peer_expert(x, h, a_t, b_t, g_t, ut, v, n_keys):
    nr = x.shape[0]
    t = PEER_TM
    n_exp = v.shape[0]
    slot = pl.BlockSpec((PEER_SLOTS, t), lambda i, j: (0, i))
    return pl.pallas_call(
        functools.partial(_peer_expert_kernel, n_keys),
        grid=(nr // t, n_exp // PEER_EC),
        in_specs=[pl.BlockSpec((t, D_MODEL), lambda i, j: (i, 0)),
                  pl.BlockSpec((t, D_MODEL), lambda i, j: (i, 0)),
                  slot, slot, slot,
                  pl.BlockSpec((D_MODEL, PEER_EC), lambda i, j: (0, j)),
                  pl.BlockSpec((PEER_EC, D_MODEL), lambda i, j: (j, 0))],
        out_specs=pl.BlockSpec((t, D_MODEL), lambda i, j: (i, 0)),
        out_shape=jax.ShapeDtypeStruct((nr, D_MODEL), F32),
        scratch_shapes=[pltpu.VMEM((t * n_keys, n_keys), F32),
                        pltpu.VMEM((t, PEER_SLOTS), F32), pltpu.VMEM((t, PEER_SLOTS), F32),
                        pltpu.VMEM((t, PEER_SLOTS), F32)],
        compiler_params=_cparams(("parallel", "arbitrary")),
        name="peer_expert",
    )(x, h, a_t, b_t, g_t, ut, v)


def _permute_w_in(w):
    small = jnp.concatenate([w[:, 3592:3600], w[:, 1536:1544]], axis=1)
    pad = jnp.zeros((w.shape[0], Z_WIDTH - Z_SM - small.shape[1]), w.dtype)
    return jnp.concatenate([w[:, 0:1536], w[:, 1544:3592], w[:, 3600:7184], small, pad],
                           axis=1).astype(BF16)


def _lane_row(values, offset):
    row = jnp.zeros((1, 128), F32)
    return row.at[0, offset:offset + values.shape[0]].set(values.astype(F32))


def _s5_weights(a_re, a_im, log_dt, b_re, b_im, c_re, c_im, d, glu_w, glu_b):
    dt = jnp.exp(log_dt)[:, None]
    mag = jnp.exp(a_re * dt)
    lam_re = mag * jnp.cos(a_im * dt)
    lam_im = mag * jnp.sin(a_im * dt)
    den = a_re * a_re + a_im * a_im
    coef_re = ((lam_re - 1.0) * a_re + lam_im * a_im) / den
    coef_im = (lam_im * a_re - (lam_re - 1.0) * a_im) / den
    bb_re = coef_re[..., None] * b_re - coef_im[..., None] * b_im
    bb_im = coef_re[..., None] * b_im + coef_im[..., None] * b_re
    eye = jnp.eye(G_C, dtype=F32)
    in_bd = lambda m: jnp.einsum('gph,gk->ghkp', m, eye).reshape(W_C, N_STATE_C).astype(BF16)
    out_bd = lambda m: jnp.einsum('ghp,gk->gpkh', m, eye).reshape(N_STATE_C, W_C).astype(BF16)
    return (in_bd(bb_re), in_bd(bb_im), lam_re.reshape(1, N_STATE_C), lam_im.reshape(1, N_STATE_C),
            out_bd(c_re), out_bd(c_im), d.reshape(1, W_C), glu_w.astype(BF16),
            glu_b.reshape(1, W_C))


def _peer_key_matrix(keys):
    h, two, nk, dk = keys.shape
    eye = jnp.eye(h * two, dtype=F32)
    kb = jnp.einsum('gkd,gf->gkfd', keys.reshape(h * two, nk, dk), eye)
    return kb.reshape(h * two * nk, h * two * dk).astype(BF16)


def kernel(x_prompt, x_sample, cache_fox_k, cache_fox_v, cache_fox_logf, page_table, state_gdn,
           state_gdn_conv, state_ssm_re, state_ssm_im, meta_tokens, norm_mix_w, w_in, gdn_conv_w,
           gdn_A_log, gdn_dt_bias, gdn_norm_w, fox_f_bias, fox_qn_w, fox_kn_w, ssm_A_re, ssm_A_im,
           ssm_log_dt, ssm_B_re, ssm_B_im, ssm_C_re, ssm_C_im, ssm_D, ssm_glu_w, ssm_glu_b,
           w_branch_a, w_branch_b, w_branch_c, w_out, norm_ffn_w, peer_wq, peer_keys, peer_u, peer_v):
    n_batch, seq, d = x_prompt.shape
    n_dec = x_sample.shape[0]
    depth = w_in.shape[0]
    n_keys = peer_keys.shape[3]
    n_phys, page = cache_fox_k.shape[1], cache_fox_k.shape[2]
    assert d == D_MODEL and x_sample.shape[1] == 1 and seq % ROW_BLOCK == 0 and page == ROW_BLOCK
    tp = seq + ROW_BLOCK
    nblk = tp // ROW_BLOCK
    n_p = n_batch * tp
    nr = -(-(n_p + n_dec) // ROW_TILE) * ROW_TILE
    s0, s1 = n_p, n_p + n_dec

    meta = jnp.broadcast_to(meta_tokens.astype(F32)[None], (n_batch, N_META, d))
    xp = jnp.concatenate([jnp.zeros((n_batch, PAD_ROWS, d), F32), meta, x_prompt], axis=1)
    x = jnp.concatenate([xp.reshape(n_p, d), x_sample[:, 0, :], jnp.zeros((nr - s1, d), F32)], axis=0)
    rows = jnp.arange(nr)
    valid = ((rows < n_p) & ((rows % tp) >= PAD_ROWS)) | ((rows >= s0) & (rows < s1))
    mask = valid.astype(F32)[:, None]

    cache_k = cache_fox_k.reshape(depth, n_phys, page, W_B)
    cache_v = cache_fox_v.reshape(depth, n_phys, page, W_B)
    cache_lt = jnp.swapaxes(cache_fox_logf, 2, 3)
    grp = jnp.kron(jnp.eye(H_B, dtype=F32), jnp.ones((DH_B, DH_B), F32))

    def prompt_rows(a, width):
        return a[:n_p].reshape(n_batch, tp, width)[:, PAD_ROWS:]

    outs_p = [[] for _ in range(7)]
    outs_s = [[] for _ in range(7)]
    for l in range(depth):
        z = _inproj(x, mask, norm_mix_w[l].reshape(1, d), _permute_w_in(w_in[l]))

        alog_row = _lane_row(gdn_A_log[l], SM_A)
        dtb_row = _lane_row(gdn_dt_bias[l], SM_A)
        gnw = gdn_norm_w[l].reshape(1, DV_A)
        alog_col = jnp.pad(gdn_A_log[l].astype(F32), (0, 8 - H_A)).reshape(8, 1)
        dtb_col = jnp.pad(gdn_dt_bias[l].astype(F32), (0, 8 - H_A)).reshape(8, 1)
        oa_p, s_p = _gdn_prompt(z, gdn_conv_w[l], alog_row, dtb_row, alog_col, dtb_col, gnw,
                                n_batch, nblk)
        qkv_s = z[s0:s1, Z_QKV:Z_QKV + QKV_A]
        oa_s, s_s = _gdn_sample(l, qkv_s, z[s0:s1, Z_ZA:Z_ZA + W_A], z[s0:s1, Z_SM:Z_SM + 128],
                                state_gdn_conv, state_gdn, gdn_conv_w[l], alog_row, dtb_row, gnw)

        qn, kn, knb, vb, logf, ccol, crow = _fox_prep(
            z, mask, jnp.tile(fox_qn_w[l], H_B).reshape(1, W_B),
            jnp.tile(fox_kn_w[l], H_B).reshape(1, W_B), _lane_row(fox_f_bias[l], SM_F), grp, nblk)
        ccol_p = ccol[:n_p].reshape(n_p, H_B // 2, 2).transpose(1, 0, 2)
        crow_p = crow[:n_batch * nblk].reshape(n_batch, nblk, H_B // 2, 2, ROW_BLOCK).transpose(
            2, 0, 1, 3, 4).reshape(H_B // 2, n_batch * nblk, 2, ROW_BLOCK)
        ob_p = _fox_prompt(qn, knb, vb, ccol_p, crow_p, n_batch, nblk)
        lf_s = jnp.pad(logf[s0:s1], ((0, 0), (0, 128 - H_B)))
        ob_s = _fox_sample(l, page_table, qn[s0:s1].astype(F32), kn[s0:s1],
                           z[s0:s1, Z_VB:Z_VB + W_B], lf_s, cache_k, cache_v, cache_lt)

        s5w = _s5_weights(ssm_A_re[l], ssm_A_im[l], ssm_log_dt[l], ssm_B_re[l], ssm_B_im[l],
                          ssm_C_re[l], ssm_C_im[l], ssm_D[l], ssm_glu_w[l], ssm_glu_b[l])
        oc_p, hre_p, him_p = _s5_prompt(z, s5w, n_batch, nblk)
        oc_s, hre_s, him_s = _s5_sample(z[s0:s1, Z_UC:Z_UC + W_C],
                                        state_ssm_re[l].reshape(n_dec, N_STATE_C),
                                        state_ssm_im[l].reshape(n_dec, N_STATE_C), s5w)

        def all_rows(p, s, width):
            return jnp.concatenate([p, s.astype(BF16), jnp.zeros((nr - s1, width), BF16)], axis=0)

        x = _merge(x, all_rows(oa_p, oa_s, W_A), all_rows(ob_p, ob_s, W_B), all_rows(oc_p, oc_s, W_C),
                   z, w_branch_a[l].astype(BF16), w_branch_b[l].astype(BF16),
                   w_branch_c[l].astype(BF16), w_out[l].astype(BF16))

        h2, a_t, b_t, g_t = _peer_route(x, norm_ffn_w[l].reshape(1, d), peer_wq[l].astype(BF16),
                                        _peer_key_matrix(peer_keys[l]), n_keys)
        x = _peer_expert(x, h2, a_t, b_t, g_t, peer_u[l].T.astype(BF16), peer_v[l].astype(BF16),
                         n_keys)

        outs_p[0].append(prompt_rows(kn, W_B).reshape(n_batch, N_META + seq, H_B, DH_B))
        outs_p[1].append(prompt_rows(z[:, Z_VB:Z_VB + W_B], W_B).reshape(
            n_batch, N_META + seq, H_B, DH_B))
        outs_p[2].append(prompt_rows(logf, H_B))
        outs_p[3].append(s_p)
        outs_p[4].append(z[:n_p, Z_QKV:Z_QKV + QKV_A].reshape(n_batch, tp, QKV_A)[:, tp - (CONV_W - 1):])
        outs_p[5].append(hre_p.reshape(n_batch, G_C, P_C))
        outs_p[6].append(him_p.reshape(n_batch, G_C, P_C))
        outs_s[0].append(kn[s0:s1].reshape(n_dec, 1, H_B, DH_B))
        outs_s[1].append(z[s0:s1, Z_VB:Z_VB + W_B].reshape(n_dec, 1, H_B, DH_B))
        outs_s[2].append(logf[s0:s1].reshape(n_dec, 1, H_B))
        outs_s[3].append(s_s)
        outs_s[4].append(jnp.concatenate([state_gdn_conv[l][:, 1:], qkv_s[:, None, :]], axis=1))
        outs_s[5].append(hre_s.reshape(n_dec, G_C, P_C))
        outs_s[6].append(him_s.reshape(n_dec, G_C, P_C))

    y_prompt = x[:n_p].reshape(n_batch, tp, d)[:, ROW_BLOCK:]
    y_sample = x[s0:s1].reshape(n_dec, 1, d)
    return (y_prompt, y_sample) + tuple(jnp.stack(o) for o in outs_p) + tuple(
        jnp.stack(o) for o in outs_s)
```
